```python
import jax, jax.numpy as jnp
from jax import lax
import numpy as np

D_MODEL = 1024
BATCH = 8
SEQ = 2048
DEPTH = 1
DEC_BATCH = 2
DEC_SEQ = 8192
PAST_LEN = 128

HEAD_DIM = 64
N_HEADS = 8
N_KV_HEADS = 2
ATTN_W = N_HEADS * HEAD_DIM
KV_W = N_KV_HEADS * HEAD_DIM
CONV_W = D_MODEL - ATTN_W
CONV_GROUPS = 8
CONV_K = 31
WINDOW = 128
BLOCK = 128
D_FF = 2816
FFN_CONV_K = 3
IN_W = ATTN_W + 2 * KV_W + 2 * CONV_W
NEG_INF = -1e30

kernel_name = "hymba_swa_conformer_convffn_encoder"


def _rms(x, g, eps=1e-6):
    xf = x.astype(jnp.float32)
    y = xf * lax.rsqrt(jnp.mean(xf * xf, axis=-1, keepdims=True) + eps)
    return (y * g.astype(jnp.float32)).astype(x.dtype)


def _layernorm(x, g, b, eps=1e-5):
    xf = x.astype(jnp.float32)
    mu = jnp.mean(xf, axis=-1, keepdims=True)
    var = jnp.mean(jnp.square(xf - mu), axis=-1, keepdims=True)
    y = (xf - mu) * lax.rsqrt(var + eps)
    return (y * g.astype(jnp.float32) + b.astype(jnp.float32)).astype(x.dtype)


def _dwconv(x, w, b):
    K, C = w.shape
    y = lax.conv_general_dilated(
        x, w[:, None, :].astype(x.dtype), window_strides=(1,),
        padding=[(K // 2, K // 2)], dimension_numbers=("NWC", "WIO", "NWC"),
        feature_group_count=C)
    return y + b.astype(x.dtype)


def _alibi_slopes(n):
    return jnp.asarray(np.array([2.0 ** (-8.0 * (h + 1) / n) for h in range(n)], dtype=np.float32))


def _window_attention(q, k, v, sink):
    B, T, H, hd = q.shape
    KVH = k.shape[2]
    G = H // KVH
    nb = T // BLOCK
    qb = q.reshape(B, nb, BLOCK, KVH, G, hd).astype(jnp.float32)

    def band(t):
        tb = t.reshape(B, nb, BLOCK, KVH, hd)
        tp = jnp.pad(tb, ((0, 0), (1, 1), (0, 0), (0, 0), (0, 0)))
        return jnp.concatenate([tp[:, :-2], tp[:, 1:-1], tp[:, 2:]], axis=2)

    kb = band(k).astype(jnp.float32)
    vb = band(v)
    s = jnp.einsum("bnqkgd,bnskd->bnkgqs", qb, kb) * (hd ** -0.5)

    blk = jnp.arange(nb)[:, None]
    qpos = blk * BLOCK + jnp.arange(BLOCK)[None, :]
    kpos = (blk - 1) * BLOCK + jnp.arange(3 * BLOCK)[None, :]
    dist = jnp.abs(qpos[:, :, None] - kpos[:, None, :])
    valid = (dist <= WINDOW) & (kpos[:, None, :] >= 0) & (kpos[:, None, :] < T)
    slopes = _alibi_slopes(H).reshape(KVH, G)
    bias = -slopes[None, :, :, None, None] * dist[:, None, None].astype(jnp.float32)
    s = jnp.where(valid[:, None, None], s + bias[None], NEG_INF)

    sink_col = jnp.broadcast_to(sink.astype(jnp.float32).reshape(1, 1, KVH, G, 1, 1), s.shape[:-1] + (1,))
    p = jax.nn.softmax(jnp.concatenate([s, sink_col], axis=-1), axis=-1)[..., :-1]
    o = jnp.einsum("bnkgqs,bnskd->bnqkgd", p.astype(v.dtype), vb)
    return o.reshape(B, T, H * hd)


def _layer(x, norm1_g, w_in, q_norm_g, k_norm_g, attn_sink, conv_dw_w, conv_dw_b,
           conv_ln_g, conv_ln_b, w_out, norm2_g, ffn_w_gate, ffn_w_up, ffn_dw_w,
           ffn_dw_b, ffn_w_down):
    B, T, _ = x.shape
    h = _rms(x, norm1_g)
    proj = h @ w_in
    q, k, v, c = jnp.split(proj, [ATTN_W, ATTN_W + KV_W, ATTN_W + 2 * KV_W], axis=-1)
    q = _rms(q.reshape(B, T, N_HEADS, HEAD_DIM), q_norm_g)
    k = _rms(k.reshape(B, T, N_KV_HEADS, HEAD_DIM), k_norm_g)
    v = v.reshape(B, T, N_KV_HEADS, HEAD_DIM)
    attn = _window_attention(q, k, v, attn_sink)
    a, gate = jnp.split(c, 2, axis=-1)
    u = a * jax.nn.sigmoid(gate)
    u = _dwconv(u, conv_dw_w, conv_dw_b)
    u = jax.nn.silu(_layernorm(u, conv_ln_g, conv_ln_b))
    x = x + jnp.concatenate([attn, u], axis=-1) @ w_out
    h2 = _rms(x, norm2_g)
    g = jax.nn.silu(_dwconv(h2 @ ffn_w_gate, ffn_dw_w, ffn_dw_b))
    x = x + (g * (h2 @ ffn_w_up)) @ ffn_w_down
    return x


def _trunk(x, norm1_g, w_in, q_norm_g, k_norm_g, attn_sink, conv_dw_w, conv_dw_b,
           conv_ln_g, conv_ln_b, w_out, norm2_g, ffn_w_gate, ffn_w_up, ffn_dw_w,
           ffn_dw_b, ffn_w_down):
    for l in range(DEPTH):
        x = _layer(x, norm1_g[l], w_in[l], q_norm_g[l], k_norm_g[l], attn_sink[l],
                   conv_dw_w[l], conv_dw_b[l], conv_ln_g[l], conv_ln_b[l], w_out[l],
                   norm2_g[l], ffn_w_gate[l], ffn_w_up[l], ffn_dw_w[l], ffn_dw_b[l],
                   ffn_w_down[l])
    return x


def setup_inputs(seed: int = 0) -> dict:
    key = jax.random.key(seed)
    ks = jax.random.split(key, 20)
    f32 = jnp.float32
    nrm = lambda k, shape, s: jax.random.normal(k, shape, dtype=f32) * s
    L = DEPTH
    return {
        "x_prompt": nrm(ks[0], (BATCH, SEQ, D_MODEL), 1.0),
        "x_sample": nrm(ks[1], (DEC_BATCH, DEC_SEQ, D_MODEL), 1.0),
        "norm1_g": 1.0 + nrm(ks[2], (L, D_MODEL), 0.02),
        "w_in": nrm(ks[3], (L, D_MODEL, IN_W), D_MODEL ** -0.5),
        "q_norm_g": 1.0 + nrm(ks[4], (L, HEAD_DIM), 0.02),
        "k_norm_g": 1.0 + nrm(ks[5], (L, HEAD_DIM), 0.02),
        "attn_sink": nrm(ks[6], (L, N_HEADS), 0.5),
        "conv_dw_w": nrm(ks[7], (L, CONV_K, CONV_W), CONV_K ** -0.5),
        "conv_dw_b": nrm(ks[8], (L, CONV_W), 0.02),
        "conv_ln_g": 1.0 + nrm(ks[9], (L, CONV_W), 0.02),
        "conv_ln_b": nrm(ks[10], (L, CONV_W), 0.02),
        "w_out": nrm(ks[11], (L, D_MODEL, D_MODEL), D_MODEL ** -0.5),
        "norm2_g": 1.0 + nrm(ks[12], (L, D_MODEL), 0.02),
        "ffn_w_gate": nrm(ks[13], (L, D_MODEL, D_FF), D_MODEL ** -0.5),
        "ffn_w_up": nrm(ks[14], (L, D_MODEL, D_FF), D_MODEL ** -0.5),
        "ffn_dw_w": nrm(ks[15], (L, FFN_CONV_K, D_FF), FFN_CONV_K ** -0.5),
        "ffn_dw_b": nrm(ks[16], (L, D_FF), 0.02),
        "ffn_w_down": nrm(ks[17], (L, D_FF, D_MODEL), D_FF ** -0.5),
    }


def reference(x_prompt, x_sample, norm1_g, w_in, q_norm_g, k_norm_g, attn_sink, conv_dw_w,
              conv_dw_b, conv_ln_g, conv_ln_b, w_out, norm2_g, ffn_w_gate, ffn_w_up,
              ffn_dw_w, ffn_dw_b, ffn_w_down):
    y_prompt = _trunk(x_prompt, norm1_g, w_in, q_norm_g, k_norm_g, attn_sink, conv_dw_w,
                      conv_dw_b, conv_ln_g, conv_ln_b, w_out, norm2_g, ffn_w_gate, ffn_w_up,
                      ffn_dw_w, ffn_dw_b, ffn_w_down)
    y_sample = _trunk(x_sample, norm1_g, w_in, q_norm_g, k_norm_g, attn_sink, conv_dw_w,
                      conv_dw_b, conv_ln_g, conv_ln_b, w_out, norm2_g, ffn_w_gate, ffn_w_up,
                      ffn_dw_w, ffn_dw_b, ffn_w_down)
    return (y_prompt, y_sample)
```

```python
import functools

import jax
import jax.numpy as jnp
import numpy as np
from jax import lax
from jax.experimental import pallas as pl
from jax.experimental.pallas import tpu as pltpu

D_MODEL = 1024
HEAD_DIM = 64
N_HEADS = 8
N_KV_HEADS = 2
ATTN_W = N_HEADS * HEAD_DIM
KV_W = N_KV_HEADS * HEAD_DIM
CONV_W = D_MODEL - ATTN_W
CONV_K = 31
WINDOW = 128
BLOCK = 128
D_FF = 2816
IN_W = ATTN_W + 2 * KV_W + 2 * CONV_W
NEG_INF = -1e30
RMS_EPS = 1e-6
LN_EPS = 1e-5

LANES = 128
HALO = 16
PROJ_ROWS = 512
MIX_ROWS = 512
FFN_ROWS = 512
FFN_CHUNK = 256
CONV_ROWS = 64
VMEM_LIMIT = 56 * 1024 * 1024

BF16 = jnp.bfloat16
F32 = jnp.float32


def _rms_rows(x, g):
    ms = jnp.mean(x * x, axis=-1, keepdims=True)
    return x * lax.rsqrt(ms + RMS_EPS) * g


def _proj_kernel(x_ref, g1_ref, win_ref, gq_ref, gk_ref, bq_ref, bk_ref,
                 q_ref, kd_ref, vd_ref, u_ref):
    h = _rms_rows(x_ref[...], g1_ref[...]).astype(BF16)
    proj = jnp.dot(h, win_ref[...], preferred_element_type=F32)

    q = proj[:, :ATTN_W]
    qms = jnp.dot((q * q).astype(BF16), bq_ref[...], preferred_element_type=F32)
    q_ref[...] = (q * lax.rsqrt(qms + RMS_EPS) * gq_ref[...]).astype(BF16)

    k = proj[:, ATTN_W:ATTN_W + KV_W]
    kms = jnp.dot((k * k).astype(BF16), bk_ref[...], preferred_element_type=F32)
    k = k * lax.rsqrt(kms + RMS_EPS) * gk_ref[...]
    v = proj[:, ATTN_W + KV_W:ATTN_W + 2 * KV_W]

    low = lax.broadcasted_iota(jnp.int32, k.shape, 1) < HEAD_DIM
    for src, dst in ((k, kd_ref), (v, vd_ref)):
        swapped = pltpu.roll(src, HEAD_DIM, axis=1)
        dst[:, :LANES] = jnp.where(low, src, swapped).astype(BF16)
        dst[:, LANES:] = jnp.where(low, swapped, src).astype(BF16)

    c0 = ATTN_W + 2 * KV_W
    a = proj[:, c0:c0 + CONV_W]
    gate = proj[:, c0 + CONV_W:]
    u_ref[...] = (a * jax.nn.sigmoid(gate)).astype(BF16)


def _proj_call(x2d, g1, win, gq, gk, bq, bk):
    n = x2d.shape[0]
    rows = PROJ_ROWS
    const = lambda i: (0, 0)
    tile = lambda i: (i, 0)
    return pl.pallas_call(
        _proj_kernel,
        grid=(n // rows,),
        in_specs=[
            pl.BlockSpec((rows, D_MODEL), tile),
            pl.BlockSpec((1, D_MODEL), const),
            pl.BlockSpec((D_MODEL, IN_W), const),
            pl.BlockSpec((1, ATTN_W), const),
            pl.BlockSpec((1, KV_W), const),
            pl.BlockSpec((ATTN_W, ATTN_W), const),
            pl.BlockSpec((KV_W, KV_W), const),
        ],
        out_specs=[
            pl.BlockSpec((rows, ATTN_W), tile),
            pl.BlockSpec((rows, 2 * KV_W), tile),
            pl.BlockSpec((rows, 2 * KV_W), tile),
            pl.BlockSpec((rows, CONV_W), tile),
        ],
        out_shape=[
            jax.ShapeDtypeStruct((n, ATTN_W), BF16),
            jax.ShapeDtypeStruct((n, 2 * KV_W), BF16),
            jax.ShapeDtypeStruct((n, 2 * KV_W), BF16),
            jax.ShapeDtypeStruct((n, CONV_W), BF16),
        ],
        compiler_params=pltpu.CompilerParams(
            dimension_semantics=("parallel",), vmem_limit_bytes=VMEM_LIMIT),
        name="proj",
    )(x2d, g1, win, gq, gk, bq, bk)


def _mixer_kernel(x_ref, q_ref, kd_ref, kdp_ref, kdn_ref, vd_ref, vdp_ref, vdn_ref,
                  u_ref, up_ref, un_ref, bias_ref, sink_ref, cw_ref, cb_ref, lg_ref, lb_ref,
                  wout_ref, o_ref, kext, vext, uext, attn_scr, conv_scr):
    rows = MIX_ROWS
    nblk = rows // BLOCK
    i = pl.program_id(1)
    has_prev = i > 0
    has_next = i < pl.num_programs(1) - 1

    kext[0:BLOCK] = kdp_ref[...]
    kext[BLOCK:BLOCK + rows] = kd_ref[...]
    kext[BLOCK + rows:] = kdn_ref[...]
    vext[0:BLOCK] = vdp_ref[...]
    vext[BLOCK:BLOCK + rows] = vd_ref[...]
    vext[BLOCK + rows:] = vdn_ref[...]

    pen_prev = jnp.where(has_prev, 0.0, NEG_INF).astype(F32)
    pen_next = jnp.where(has_next, 0.0, NEG_INF).astype(F32)

    low = lax.broadcasted_iota(jnp.int32, (BLOCK, LANES), 1) < HEAD_DIM
    zero_bf = jnp.zeros((BLOCK, LANES), BF16)

    for j in range(nblk):
        for g in range(N_KV_HEADS):
            kwin = kext[j * BLOCK:(j + 3) * BLOCK, g * LANES:(g + 1) * LANES]
            vwin = vext[j * BLOCK:(j + 3) * BLOCK, g * LANES:(g + 1) * LANES]
            for m in range(2):
                col = (2 * g + m) * LANES
                q128 = q_ref[j * BLOCK:(j + 1) * BLOCK, col:col + LANES]
                q2 = jnp.concatenate(
                    [jnp.where(low, q128, zero_bf), jnp.where(low, zero_bf, q128)], axis=0)
                s = lax.dot_general(q2, kwin, (((1,), (1,)), ((), ())),
                                    preferred_element_type=F32)
                outs = []
                for half in range(2):
                    head = 4 * g + 2 * m + half
                    sh = s[half * BLOCK:(half + 1) * BLOCK] + bias_ref[head]
                    if j == 0:
                        edge = lax.broadcasted_iota(jnp.int32, sh.shape, 1) < BLOCK
                        sh = sh + jnp.where(edge, pen_prev, 0.0)
                    if j == nblk - 1:
                        edge = lax.broadcasted_iota(jnp.int32, sh.shape, 1) >= 2 * BLOCK
                        sh = sh + jnp.where(edge, pen_next, 0.0)
                    sink = sink_ref[head]
                    mx = jnp.maximum(jnp.max(sh, axis=-1, keepdims=True), sink)
                    e = jnp.exp(sh - mx)
                    denom = jnp.sum(e, axis=-1, keepdims=True) + jnp.exp(sink - mx)
                    pv = jnp.dot(e.astype(BF16), vwin, preferred_element_type=F32)
                    outs.append(pv / denom)
                attn_scr[j * BLOCK:(j + 1) * BLOCK, col:col + LANES] = (
                    jnp.where(low, outs[0], outs[1]).astype(BF16))

    uext[0:HALO] = jnp.where(has_prev, up_ref[...], jnp.zeros_like(up_ref)).astype(F32)
    uext[HALO:HALO + rows] = u_ref[...].astype(F32)
    uext[HALO + rows:] = jnp.where(has_next, un_ref[...], jnp.zeros_like(un_ref)).astype(F32)
    first = HALO - CONV_K // 2
    for r in range(rows // CONV_ROWS):
        for c in range(CONV_W // LANES):
            cs = slice(c * LANES, (c + 1) * LANES)
            acc = jnp.broadcast_to(cb_ref[:, cs], (CONV_ROWS, LANES))
            for k in range(CONV_K):
                start = r * CONV_ROWS + first + k
                acc = acc + cw_ref[k:k + 1, cs] * uext[start:start + CONV_ROWS, cs]
            conv_scr[r * CONV_ROWS:(r + 1) * CONV_ROWS, cs] = acc

    y = conv_scr[...]
    mu = jnp.mean(y, axis=-1, keepdims=True)
    yc = y - mu
    var = jnp.mean(yc * yc, axis=-1, keepdims=True)
    y = yc * lax.rsqrt(var + LN_EPS) * lg_ref[...] + lb_ref[...]
    u2 = (y * jax.nn.sigmoid(y)).astype(BF16)

    o_ref[...] = (x_ref[...]
                  + jnp.dot(attn_scr[...], wout_ref[:ATTN_W, :], preferred_element_type=F32)
                  + jnp.dot(u2, wout_ref[ATTN_W:, :], preferred_element_type=F32))


def _mixer_call(x2d, q, kd, vd, u, bias, sink, cw, cb, lg, lb, wout, batch, seq):
    rows = MIX_ROWS
    nt = seq // rows
    per_blk = rows // BLOCK
    per_halo = rows // HALO
    nblk_seq = seq // BLOCK
    nhalo_seq = seq // HALO

    tile = lambda b, i: (b * nt + i, 0)
    blk_prev = lambda b, i: (b * nblk_seq + jnp.maximum(i * per_blk - 1, 0), 0)
    blk_next = lambda b, i: (b * nblk_seq + jnp.minimum((i + 1) * per_blk, nblk_seq - 1), 0)
    halo_prev = lambda b, i: (b * nhalo_seq + jnp.maximum(i * per_halo - 1, 0), 0)
    halo_next = lambda b, i: (b * nhalo_seq + jnp.minimum((i + 1) * per_halo, nhalo_seq - 1), 0)
    const2 = lambda b, i: (0, 0)
    const3 = lambda b, i: (0, 0, 0)

    return pl.pallas_call(
        _mixer_kernel,
        grid=(batch, nt),
        in_specs=[
            pl.BlockSpec((rows, D_MODEL), tile),
            pl.BlockSpec((rows, ATTN_W), tile),
            pl.BlockSpec((rows, 2 * KV_W), tile),
            pl.BlockSpec((BLOCK, 2 * KV_W), blk_prev),
            pl.BlockSpec((BLOCK, 2 * KV_W), blk_next),
            pl.BlockSpec((rows, 2 * KV_W), tile),
            pl.BlockSpec((BLOCK, 2 * KV_W), blk_prev),
            pl.BlockSpec((BLOCK, 2 * KV_W), blk_next),
            pl.BlockSpec((rows, CONV_W), tile),
            pl.BlockSpec((HALO, CONV_W), halo_prev),
            pl.BlockSpec((HALO, CONV_W), halo_next),
            pl.BlockSpec((N_HEADS, BLOCK, 3 * BLOCK), const3),
            pl.BlockSpec(memory_space=pltpu.SMEM),
            pl.BlockSpec((CONV_K + 1, CONV_W), const2),
            pl.BlockSpec((1, CONV_W), const2),
            pl.BlockSpec((1, CONV_W), const2),
            pl.BlockSpec((1, CONV_W), const2),
            pl.BlockSpec((D_MODEL, D_MODEL), const2),
        ],
        out_specs=pl.BlockSpec((rows, D_MODEL), tile),
        out_shape=jax.ShapeDtypeStruct(x2d.shape, F32),
        scratch_shapes=[
            pltpu.VMEM((rows + 2 * BLOCK, 2 * KV_W), BF16),
            pltpu.VMEM((rows + 2 * BLOCK, 2 * KV_W), BF16),
            pltpu.VMEM((rows + 2 * HALO, CONV_W), F32),
            pltpu.VMEM((rows, ATTN_W), BF16),
            pltpu.VMEM((rows, CONV_W), F32),
        ],
        compiler_params=pltpu.CompilerParams(
            dimension_semantics=("parallel", "parallel"), vmem_limit_bytes=VMEM_LIMIT),
        name="mixer",
    )(x2d, q, kd, kd, kd, vd, vd, vd, u, u, u, bias, sink, cw, cb, lg, lb, wout)


def _ffn_kernel(x_ref, xp_ref, xn_ref, g2_ref, wg_ref, wu_ref, wd_ref, dw_ref, db_ref,
                o_ref, h_scr, ge_scr, a_scr):
    rows = FFN_ROWS
    i = pl.program_id(1)
    has_prev = i > 0
    has_next = i < pl.num_programs(1) - 1
    g2 = g2_ref[...]

    hp = _rms_rows(xp_ref[...], g2)
    hn = _rms_rows(xn_ref[...], g2)
    h_scr[0:HALO] = jnp.where(has_prev, hp, 0.0).astype(BF16)
    h_scr[HALO:HALO + rows] = _rms_rows(x_ref[...], g2).astype(BF16)
    h_scr[HALO + rows:] = jnp.where(has_next, hn, 0.0).astype(BF16)

    for c in range(D_FF // FFN_CHUNK):
        cs = slice(c * FFN_CHUNK, (c + 1) * FFN_CHUNK)
        ge_scr[...] = jnp.dot(h_scr[...], wg_ref[:, cs], preferred_element_type=F32)
        up = jnp.dot(h_scr[HALO:HALO + rows], wu_ref[:, cs], preferred_element_type=F32)
        gc = (dw_ref[0:1, cs] * ge_scr[HALO - 1:HALO - 1 + rows]
              + dw_ref[1:2, cs] * ge_scr[HALO:HALO + rows]
              + dw_ref[2:3, cs] * ge_scr[HALO + 1:HALO + 1 + rows]
              + db_ref[:, cs])
        a_scr[:, cs] = (gc * jax.nn.sigmoid(gc) * up).astype(BF16)

    o_ref[...] = x_ref[...] + jnp.dot(a_scr[...], wd_ref[...], preferred_element_type=F32)


def _ffn_call(x2d, g2, wg, wu, wd, dw, db, batch, seq):
    rows = FFN_ROWS
    nt = seq // rows
    per_halo = rows // HALO
    nhalo_seq = seq // HALO
    tile = lambda b, i: (b * nt + i, 0)
    halo_prev = lambda b, i: (b * nhalo_seq + jnp.maximum(i * per_halo - 1, 0), 0)
    halo_next = lambda b, i: (b * nhalo_seq + jnp.minimum((i + 1) * per_halo, nhalo_seq - 1), 0)
    const2 = lambda b, i: (0, 0)
    resident = functools.partial(pl.BlockSpec, index_map=const2, pipeline_mode=pl.Buffered(1))

    return pl.pallas_call(
        _ffn_kernel,
        grid=(batch, nt),
        in_specs=[
            pl.BlockSpec((rows, D_MODEL), tile),
            pl.BlockSpec((HALO, D_MODEL), halo_prev),
            pl.BlockSpec((HALO, D_MODEL), halo_next),
            pl.BlockSpec((1, D_MODEL), const2),
            resident((D_MODEL, D_FF)),
            resident((D_MODEL, D_FF)),
            resident((D_FF, D_MODEL)),
            pl.BlockSpec((8, D_FF), const2),
            pl.BlockSpec((1, D_FF), const2),
        ],
        out_specs=pl.BlockSpec((rows, D_MODEL), tile),
        out_shape=jax.ShapeDtypeStruct(x2d.shape, F32),
        scratch_shapes=[
            pltpu.VMEM((rows + 2 * HALO, D_MODEL), BF16),
            pltpu.VMEM((rows + 2 * HALO, FFN_CHUNK), F32),
            pltpu.VMEM((rows, D_FF), BF16),
        ],
        compiler_params=pltpu.CompilerParams(
            dimension_semantics=("parallel", "parallel"), vmem_limit_bytes=VMEM_LIMIT),
        name="ffn",
    )(x2d, x2d, x2d, g2, wg, wu, wd, dw, db)


def _alibi_bias():
    slopes = np.array([2.0 ** (-8.0 * (h + 1) / N_HEADS) for h in range(N_HEADS)], np.float32)
    qpos = BLOCK + np.arange(BLOCK)[:, None]
    kpos = np.arange(3 * BLOCK)[None, :]
    dist = np.abs(qpos - kpos)
    bias = -slopes[:, None, None] * dist[None].astype(np.float32)
    return jnp.asarray(np.where(dist[None] <= WINDOW, bias, np.float32(NEG_INF)), F32)


def _block_mean(width):
    idx = np.arange(width) // HEAD_DIM
    return jnp.asarray((idx[:, None] == idx[None, :]).astype(np.float32) / HEAD_DIM, BF16)


def _layer(x, layer_params, consts):
    (g1, win, gq, gk, sink, cw, cb, lg, lb, wout, g2, wg, wu, dw, db, wd) = layer_params
    bias, bq, bk = consts
    batch, seq, _ = x.shape
    x2d = x.reshape(batch * seq, D_MODEL)
    q, kd, vd, u = _proj_call(x2d, g1, win, gq, gk, bq, bk)
    x1 = _mixer_call(x2d, q, kd, vd, u, bias, sink, cw, cb, lg, lb, wout, batch, seq)
    y = _ffn_call(x1, g2, wg, wu, wd, dw, db, batch, seq)
    return y.reshape(batch, seq, D_MODEL)


def kernel(x_prompt, x_sample, norm1_g, w_in, q_norm_g, k_norm_g, attn_sink, conv_dw_w, conv_dw_b,
           conv_ln_g, conv_ln_b, w_out, norm2_g, ffn_w_gate, ffn_w_up, ffn_dw_w, ffn_dw_b,
           ffn_w_down):
    depth = norm1_g.shape[0]
    consts = (_alibi_bias(), _block_mean(ATTN_W), _block_mean(KV_W))
    layers = []
    for l in range(depth):
        layers.append((
            norm1_g[l].reshape(1, D_MODEL),
            w_in[l].astype(BF16),
            (jnp.tile(q_norm_g[l], N_HEADS) * (HEAD_DIM ** -0.5)).reshape(1, ATTN_W),
            jnp.tile(k_norm_g[l], N_KV_HEADS).reshape(1, KV_W),
            attn_sink[l],
            jnp.pad(conv_dw_w[l], ((0, 1), (0, 0))),
            conv_dw_b[l].reshape(1, CONV_W),
            conv_ln_g[l].reshape(1, CONV_W),
            conv_ln_b[l].reshape(1, CONV_W),
            w_out[l].astype(BF16),
            norm2_g[l].reshape(1, D_MODEL),
            ffn_w_gate[l].astype(BF16),
            ffn_w_up[l].astype(BF16),
            jnp.pad(ffn_dw_w[l], ((0, 5), (0, 0))),
            ffn_dw_b[l].reshape(1, D_FF),
            ffn_w_down[l].astype(BF16),
        ))
    outs = []
    for x in (x_prompt, x_sample):
        for params in layers:
            x = _layer(x, params, consts)
        outs.append(x)
    return tuple(outs)
```

```python
import functools

import jax
import jax.numpy as jnp
import numpy as np
from jax import lax
from jax.experimental import pallas as pl
from jax.experimental.pallas import tpu as pltpu

D_MODEL = 1024
HEAD_DIM = 64
N_HEADS = 8
N_KV_HEADS = 2
ATTN_W = N_HEADS * HEAD_DIM
KV_W = N_KV_HEADS * HEAD_DIM
CONV_W = D_MODEL - ATTN_W
CONV_K = 31
WINDOW = 128
BLOCK = 128
D_FF = 2816
IN_W = ATTN_W + 2 * KV_W + 2 * CONV_W
MASKED = -(2.0 ** 100)
RMS_EPS = 1e-6
LN_EPS = 1e-5

LANES = 128
HALO = 16
PROJ_ROWS = 512
MIX_ROWS = 512
FFN_ROWS = 512
FFN_CHUNK = 256
CONV_ROWS = 128
VMEM_LIMIT = 56 * 1024 * 1024

BF16 = jnp.bfloat16
F32 = jnp.float32


def _rms_rows(x, g):
    ms = jnp.mean(x * x, axis=-1, keepdims=True)
    return x * lax.rsqrt(ms + RMS_EPS) * g


def _proj_kernel(x_ref, g1_ref, win_ref, gq_ref, gk_ref, bq_ref, bk_ref,
                 q_ref, kd_ref, vd_ref, u_ref):
    h = _rms_rows(x_ref[...], g1_ref[...]).astype(BF16)
    proj = jnp.dot(h, win_ref[...], preferred_element_type=F32)

    q = proj[:, :ATTN_W]
    qms = jnp.dot((q * q).astype(BF16), bq_ref[...], preferred_element_type=F32)
    q_ref[...] = (q * lax.rsqrt(qms + RMS_EPS) * gq_ref[...]).astype(BF16)

    k = proj[:, ATTN_W:ATTN_W + KV_W]
    kms = jnp.dot((k * k).astype(BF16), bk_ref[...], preferred_element_type=F32)
    k = k * lax.rsqrt(kms + RMS_EPS) * gk_ref[...]
    v = proj[:, ATTN_W + KV_W:ATTN_W + 2 * KV_W]

    low = lax.broadcasted_iota(jnp.int32, k.shape, 1) < HEAD_DIM
    for src, dst in ((k, kd_ref), (v, vd_ref)):
        swapped = pltpu.roll(src, HEAD_DIM, axis=1)
        dst[:, :LANES] = jnp.where(low, src, swapped).astype(BF16)
        dst[:, LANES:] = jnp.where(low, swapped, src).astype(BF16)

    c0 = ATTN_W + 2 * KV_W
    a = proj[:, c0:c0 + CONV_W]
    gate = proj[:, c0 + CONV_W:]
    u_ref[...] = (a * jax.nn.sigmoid(gate)).astype(BF16)


def _proj_call(x2d, g1, win, gq, gk, bq, bk):
    n = x2d.shape[0]
    rows = PROJ_ROWS
    const = lambda i: (0, 0)
    tile = lambda i: (i, 0)
    return pl.pallas_call(
        _proj_kernel,
        grid=(n // rows,),
        in_specs=[
            pl.BlockSpec((rows, D_MODEL), tile),
            pl.BlockSpec((1, D_MODEL), const),
            pl.BlockSpec((D_MODEL, IN_W), const),
            pl.BlockSpec((1, ATTN_W), const),
            pl.BlockSpec((1, KV_W), const),
            pl.BlockSpec((ATTN_W, ATTN_W), const),
            pl.BlockSpec((KV_W, KV_W), const),
        ],
        out_specs=[
            pl.BlockSpec((rows, ATTN_W), tile),
            pl.BlockSpec((rows, 2 * KV_W), tile),
            pl.BlockSpec((rows, 2 * KV_W), tile),
            pl.BlockSpec((rows, CONV_W), tile),
        ],
        out_shape=[
            jax.ShapeDtypeStruct((n, ATTN_W), BF16),
            jax.ShapeDtypeStruct((n, 2 * KV_W), BF16),
            jax.ShapeDtypeStruct((n, 2 * KV_W), BF16),
            jax.ShapeDtypeStruct((n, CONV_W), BF16),
        ],
        compiler_params=pltpu.CompilerParams(
            dimension_semantics=("parallel",), vmem_limit_bytes=VMEM_LIMIT),
        name="proj",
    )(x2d, g1, win, gq, gk, bq, bk)


def _mixer_kernel(x_ref, q_ref, kd_ref, kdp_ref, kdn_ref, vd_ref, vdp_ref, vdn_ref,
                  u_ref, up_ref, un_ref, eye_ref, nd_ref, sink_ref, cw_ref, cb_ref, lg_ref, lb_ref,
                  wout_ref, o_ref, kext, vext, uext, attn_scr, conv_scr):
    rows = MIX_ROWS
    nblk = rows // BLOCK
    i = pl.program_id(1)
    has_prev = i > 0
    has_next = i < pl.num_programs(1) - 1

    kext[0:BLOCK] = kdp_ref[...]
    kext[BLOCK:BLOCK + rows] = kd_ref[...]
    kext[BLOCK + rows:] = kdn_ref[...]
    vext[0:BLOCK] = vdp_ref[...]
    vext[BLOCK:BLOCK + rows] = vd_ref[...]
    vext[BLOCK + rows:] = vdn_ref[...]

    nd = nd_ref[...]
    masked = jnp.full((BLOCK, LANES), MASKED, BF16)
    nd_first = jnp.concatenate([jnp.where(has_prev, nd[:BLOCK], masked), nd[BLOCK:]], axis=0)
    nd_last = jnp.concatenate([nd[:2 * BLOCK], jnp.where(has_next, nd[2 * BLOCK:], masked)],
                              axis=0)

    low = lax.broadcasted_iota(jnp.int32, (BLOCK, LANES), 1) < HEAD_DIM
    zero_bf = jnp.zeros((BLOCK, LANES), BF16)

    for j in range(nblk):
        ndj = nd_first if j == 0 else (nd_last if j == nblk - 1 else nd)
        for g in range(N_KV_HEADS):
            kwin = jnp.concatenate(
                [kext[j * BLOCK:(j + 3) * BLOCK, g * LANES:(g + 1) * LANES], ndj], axis=1)
            vwin = vext[j * BLOCK:(j + 3) * BLOCK, g * LANES:(g + 1) * LANES]
            for m in range(2):
                head = 4 * g + 2 * m
                col = (2 * g + m) * LANES
                q128 = q_ref[j * BLOCK:(j + 1) * BLOCK, col:col + LANES]
                q2 = jnp.concatenate(
                    [jnp.where(low, q128, zero_bf), jnp.where(low, zero_bf, q128)], axis=0)
                q2 = jnp.concatenate([q2, eye_ref[head * BLOCK:(head + 2) * BLOCK, :]], axis=1)
                s = lax.dot_general(q2, kwin, (((1,), (1,)), ((), ())),
                                    preferred_element_type=F32)
                es, inv = [], []
                for half in range(2):
                    sh = s[half * BLOCK:(half + 1) * BLOCK]
                    sink = sink_ref[head + half]
                    mx = jnp.maximum(jnp.max(sh, axis=-1, keepdims=True), sink)
                    e = jnp.exp(sh - mx)
                    inv.append(1.0 / (jnp.sum(e, axis=-1, keepdims=True) + jnp.exp(sink - mx)))
                    es.append(e.astype(BF16))
                pv = jnp.dot(jnp.concatenate(es, axis=0), vwin, preferred_element_type=F32)
                attn_scr[j * BLOCK:(j + 1) * BLOCK, col:col + LANES] = jnp.where(
                    low, pv[:BLOCK] * inv[0], pv[BLOCK:] * inv[1]).astype(BF16)

    uext[0:HALO] = jnp.where(has_prev, up_ref[...], jnp.zeros_like(up_ref)).astype(F32)
    uext[HALO:HALO + rows] = u_ref[...].astype(F32)
    uext[HALO + rows:] = jnp.where(has_next, un_ref[...], jnp.zeros_like(un_ref)).astype(F32)
    for r in range(rows // CONV_ROWS):
        for c in range(CONV_W // LANES):
            cs = slice(c * LANES, (c + 1) * LANES)
            acc = None
            for s in range(8):
                z = None
                for a in range(4):
                    k = 8 * a + s - (HALO - CONV_K // 2)
                    if 0 <= k < CONV_K:
                        start = r * CONV_ROWS + 8 * a
                        t = cw_ref[k:k + 1, cs] * uext[start:start + CONV_ROWS + 8, cs]
                        z = t if z is None else z + t
                z = z[s:s + CONV_ROWS]
                acc = z if acc is None else acc + z
            conv_scr[r * CONV_ROWS:(r + 1) * CONV_ROWS, cs] = acc + cb_ref[:, cs]

    y = conv_scr[...]
    mu = jnp.mean(y, axis=-1, keepdims=True)
    yc = y - mu
    var = jnp.mean(yc * yc, axis=-1, keepdims=True)
    y = yc * lax.rsqrt(var + LN_EPS) * lg_ref[...] + lb_ref[...]
    u2 = (y * jax.nn.sigmoid(y)).astype(BF16)

    o_ref[...] = (x_ref[...]
                  + jnp.dot(attn_scr[...], wout_ref[:ATTN_W, :], preferred_element_type=F32)
                  + jnp.dot(u2, wout_ref[ATTN_W:, :], preferred_element_type=F32))


def _mixer_call(x2d, q, kd, vd, u, eye, nd, sink, cw, cb, lg, lb, wout, batch, seq):
    rows = MIX_ROWS
    nt = seq // rows
    per_blk = rows // BLOCK
    per_halo = rows // HALO
    nblk_seq = seq // BLOCK
    nhalo_seq = seq // HALO

    tile = lambda b, i: (b * nt + i, 0)
    blk_prev = lambda b, i: (b * nblk_seq + jnp.maximum(i * per_blk - 1, 0), 0)
    blk_next = lambda b, i: (b * nblk_seq + jnp.minimum((i + 1) * per_blk, nblk_seq - 1), 0)
    halo_prev = lambda b, i: (b * nhalo_seq + jnp.maximum(i * per_halo - 1, 0), 0)
    halo_next = lambda b, i: (b * nhalo_seq + jnp.minimum((i + 1) * per_halo, nhalo_seq - 1), 0)
    const2 = lambda b, i: (0, 0)

    return pl.pallas_call(
        _mixer_kernel,
        grid=(batch, nt),
        in_specs=[
            pl.BlockSpec((rows, D_MODEL), tile),
            pl.BlockSpec((rows, ATTN_W), tile),
            pl.BlockSpec((rows, 2 * KV_W), tile),
            pl.BlockSpec((BLOCK, 2 * KV_W), blk_prev),
            pl.BlockSpec((BLOCK, 2 * KV_W), blk_next),
            pl.BlockSpec((rows, 2 * KV_W), tile),
            pl.BlockSpec((BLOCK, 2 * KV_W), blk_prev),
            pl.BlockSpec((BLOCK, 2 * KV_W), blk_next),
            pl.BlockSpec((rows, CONV_W), tile),
            pl.BlockSpec((HALO, CONV_W), halo_prev),
            pl.BlockSpec((HALO, CONV_W), halo_next),
            pl.BlockSpec((N_HEADS * BLOCK, LANES), const2),
            pl.BlockSpec((3 * BLOCK, LANES), const2),
            pl.BlockSpec(memory_space=pltpu.SMEM),
            pl.BlockSpec((CONV_K + 1, CONV_W), const2),
            pl.BlockSpec((1, CONV_W), const2),
            pl.BlockSpec((1, CONV_W), const2),
            pl.BlockSpec((1, CONV_W), const2),
            pl.BlockSpec((D_MODEL, D_MODEL), const2),
        ],
        out_specs=pl.BlockSpec((rows, D_MODEL), tile),
        out_shape=jax.ShapeDtypeStruct(x2d.shape, F32),
        scratch_shapes=[
            pltpu.VMEM((rows + 2 * BLOCK, 2 * KV_W), BF16),
            pltpu.VMEM((rows + 2 * BLOCK, 2 * KV_W), BF16),
            pltpu.VMEM((rows + 2 * HALO, CONV_W), F32),
            pltpu.VMEM((rows, ATTN_W), BF16),
            pltpu.VMEM((rows, CONV_W), F32),
        ],
        compiler_params=pltpu.CompilerParams(
            dimension_semantics=("parallel", "parallel"), vmem_limit_bytes=VMEM_LIMIT),
        name="mixer",
    )(x2d, q, kd, kd, kd, vd, vd, vd, u, u, u, eye, nd, sink, cw, cb, lg, lb, wout)


def _ffn_kernel(x_ref, xp_ref, xn_ref, g2_ref, wg_ref, wu_ref, wd_ref, dw_ref, db_ref,
                o_ref, h_scr, ge_scr, a_scr):
    rows = FFN_ROWS
    i = pl.program_id(1)
    has_prev = i > 0
    has_next = i < pl.num_programs(1) - 1
    g2 = g2_ref[...]

    hp = _rms_rows(xp_ref[...], g2)
    hn = _rms_rows(xn_ref[...], g2)
    h_scr[0:HALO] = jnp.where(has_prev, hp, 0.0).astype(BF16)
    h_scr[HALO:HALO + rows] = _rms_rows(x_ref[...], g2).astype(BF16)
    h_scr[HALO + rows:] = jnp.where(has_next, hn, 0.0).astype(BF16)

    for c in range(D_FF // FFN_CHUNK):
        cs = slice(c * FFN_CHUNK, (c + 1) * FFN_CHUNK)
        ge_scr[...] = jnp.dot(h_scr[...], wg_ref[:, cs], preferred_element_type=F32)
        up = jnp.dot(h_scr[HALO:HALO + rows], wu_ref[:, cs], preferred_element_type=F32)
        gc = (dw_ref[0:1, cs] * ge_scr[HALO - 1:HALO - 1 + rows]
              + dw_ref[1:2, cs] * ge_scr[HALO:HALO + rows]
              + dw_ref[2:3, cs] * ge_scr[HALO + 1:HALO + 1 + rows]
              + db_ref[:, cs])
        a_scr[:, cs] = (gc * jax.nn.sigmoid(gc) * up).astype(BF16)

    o_ref[...] = x_ref[...] + jnp.dot(a_scr[...], wd_ref[...], preferred_element_type=F32)


def _ffn_call(x2d, g2, wg, wu, wd, dw, db, batch, seq):
    rows = FFN_ROWS
    nt = seq // rows
    per_halo = rows // HALO
    nhalo_seq = seq // HALO
    tile = lambda b, i: (b * nt + i, 0)
    halo_prev = lambda b, i: (b * nhalo_seq + jnp.maximum(i * per_halo - 1, 0), 0)
    halo_next = lambda b, i: (b * nhalo_seq + jnp.minimum((i + 1) * per_halo, nhalo_seq - 1), 0)
    const2 = lambda b, i: (0, 0)
    resident = functools.partial(pl.BlockSpec, index_map=const2, pipeline_mode=pl.Buffered(1))

    return pl.pallas_call(
        _ffn_kernel,
        grid=(batch, nt),
        in_specs=[
            pl.BlockSpec((rows, D_MODEL), tile),
            pl.BlockSpec((HALO, D_MODEL), halo_prev),
            pl.BlockSpec((HALO, D_MODEL), halo_next),
            pl.BlockSpec((1, D_MODEL), const2),
            resident((D_MODEL, D_FF)),
            resident((D_MODEL, D_FF)),
            resident((D_FF, D_MODEL)),
            pl.BlockSpec((8, D_FF), const2),
            pl.BlockSpec((1, D_FF), const2),
        ],
        out_specs=pl.BlockSpec((rows, D_MODEL), tile),
        out_shape=jax.ShapeDtypeStruct(x2d.shape, F32),
        scratch_shapes=[
            pltpu.VMEM((rows + 2 * HALO, D_MODEL), BF16),
            pltpu.VMEM((rows + 2 * HALO, FFN_CHUNK), F32),
            pltpu.VMEM((rows, D_FF), BF16),
        ],
        compiler_params=pltpu.CompilerParams(
            dimension_semantics=("parallel", "parallel"), vmem_limit_bytes=VMEM_LIMIT),
        name="ffn",
    )(x2d, x2d, x2d, g2, wg, wu, wd, dw, db)


def _alibi_operands():
    slopes = np.array([2.0 ** (-8.0 * (h + 1) / N_HEADS) for h in range(N_HEADS)], np.float32)
    eye = slopes[:, None, None] * np.eye(BLOCK, dtype=np.float32)[None]
    kpos = np.arange(3 * BLOCK)[:, None]
    qpos = BLOCK + np.arange(BLOCK)[None, :]
    dist = np.abs(qpos - kpos).astype(np.float32)
    negdist = np.where(dist <= WINDOW, -dist, np.float32(MASKED))
    return (jnp.asarray(eye.reshape(N_HEADS * BLOCK, BLOCK), BF16), jnp.asarray(negdist, BF16))


def _block_mean(width):
    idx = np.arange(width) // HEAD_DIM
    return jnp.asarray((idx[:, None] == idx[None, :]).astype(np.float32) / HEAD_DIM, BF16)


def _layer(x, layer_params, consts):
    (g1, win, gq, gk, sink, cw, cb, lg, lb, wout, g2, wg, wu, dw, db, wd) = layer_params
    eye, nd, bq, bk = consts
    batch, seq, _ = x.shape
    x2d = x.reshape(batch * seq, D_MODEL)
    q, kd, vd, u = _proj_call(x2d, g1, win, gq, gk, bq, bk)
    x1 = _mixer_call(x2d, q, kd, vd, u, eye, nd, sink, cw, cb, lg, lb, wout, batch, seq)
    y = _ffn_call(x1, g2, wg, wu, wd, dw, db, batch, seq)
    return y.reshape(batch, seq, D_MODEL)


def kernel(x_prompt, x_sample, norm1_g, w_in, q_norm_g, k_norm_g, attn_sink, conv_dw_w, conv_dw_b,
           conv_ln_g, conv_ln_b, w_out, norm2_g, ffn_w_gate, ffn_w_up, ffn_dw_w, ffn_dw_b,
           ffn_w_down):
    depth = norm1_g.shape[0]
    consts = (*_alibi_operands(), _block_mean(ATTN_W), _block_mean(KV_W))
    layers = []
    for l in range(depth):
        layers.append((
            norm1_g[l].reshape(1, D_MODEL),
            w_in[l].astype(BF16),
            (jnp.tile(q_norm_g[l], N_HEADS) * (HEAD_DIM ** -0.5)).reshape(1, ATTN_W),
            jnp.tile(k_norm_g[l], N_KV_HEADS).reshape(1, KV_W),
            attn_sink[l],
            jnp.pad(conv_dw_w[l], ((0, 1), (0, 0))),
            conv_dw_b[l].reshape(1, CONV_W),
            conv_ln_g[l].reshape(1, CONV_W),
            conv_ln_b[l].reshape(1, CONV_W),
            w_out[l].astype(BF16),
            norm2_g[l].reshape(1, D_MODEL),
            ffn_w_gate[l].astype(BF16),
            ffn_w_up[l].astype(BF16),
            jnp.pad(ffn_dw_w[l], ((0, 5), (0, 0))),
            ffn_dw_b[l].reshape(1, D_FF),
            ffn_w_down[l].astype(BF16),
        ))
    outs = []
    for x in (x_prompt, x_sample):
        for params in layers:
            x = _layer(x, params, consts)
        outs.append(x)
    return tuple(outs)
```
